```python
import math
import jax, jax.numpy as jnp
from jax import lax
import numpy as np

D_MODEL = 4096
BATCH = 8
SEQ = 2048
DEPTH = 1
DEC_BATCH = 1
DEC_SEQ = 16384
PAST_LEN = 128

MIX_WIDTH = D_MODEL
ATTN_WIDTH = MIX_WIDTH // 2
HGRN_WIDTH = MIX_WIDTH - ATTN_WIDTH
DIFF_HEAD_DIM = 128
N_DIFF_HEADS = ATTN_WIDTH // (2 * DIFF_HEAD_DIM)
HGRN_HEAD_DIM = 128
N_HGRN_HEADS = HGRN_WIDTH // HGRN_HEAD_DIM
D_FF = 4 * D_MODEL
ROPE_THETA = 10000.0
NORM_EPS = 1e-6
Q_BLOCK = 128
CHUNK = 64
IN_PROJ_WIDTH = 3 * ATTN_WIDTH + 5 * HGRN_WIDTH

kernel_name = 'hymba_diffattn_hgrn2_encoder'


def rms_norm(x, w):
    xf = x.astype(jnp.float32)
    y = xf * lax.rsqrt(jnp.mean(xf * xf, axis=-1, keepdims=True) + NORM_EPS)
    return (y * w.astype(jnp.float32)).astype(x.dtype)


def rope_tables(length):
    inv = 1.0 / (ROPE_THETA ** (jnp.arange(0, DIFF_HEAD_DIM, 2, dtype=jnp.float32) / DIFF_HEAD_DIM))
    ang = jnp.arange(length, dtype=jnp.float32)[:, None] * inv[None, :]
    ang = jnp.concatenate([ang, ang], axis=-1)
    return jnp.cos(ang), jnp.sin(ang)


def apply_rope(x, cos, sin):
    half = DIFF_HEAD_DIM // 2
    x1, x2 = x[..., :half], x[..., half:]
    rot = jnp.concatenate([-x2, x1], axis=-1)
    c = cos[None, :, None, None, :]
    s = sin[None, :, None, None, :]
    return (x.astype(jnp.float32) * c + rot.astype(jnp.float32) * s).astype(x.dtype)


def diff_attention(q_a, k_a, v_a, lam, cos, sin):
    B, L, _ = q_a.shape
    q = apply_rope(q_a.reshape(B, L, N_DIFF_HEADS, 2, DIFF_HEAD_DIM), cos, sin)
    k = apply_rope(k_a.reshape(B, L, N_DIFF_HEADS, 2, DIFF_HEAD_DIM), cos, sin)
    v = v_a.reshape(B, L, N_DIFF_HEADS, 2 * DIFF_HEAD_DIM)
    n_blocks = L // Q_BLOCK
    q_blocks = q.reshape(B, n_blocks, Q_BLOCK, N_DIFF_HEADS, 2, DIFF_HEAD_DIM).transpose(1, 0, 2, 3, 4, 5)
    scale = DIFF_HEAD_DIM ** -0.5

    def attend(qb):
        s = jnp.einsum('bqhcd,bkhcd->bhcqk', qb, k).astype(jnp.float32) * scale
        p = jax.nn.softmax(s, axis=-1)
        p_diff = p[:, :, 0] - lam * p[:, :, 1]
        return jnp.einsum('bhqk,bkhe->bqhe', p_diff.astype(v.dtype), v)

    o = lax.map(attend, q_blocks)
    return o.transpose(1, 0, 2, 3, 4).reshape(B, L, N_DIFF_HEADS, 2 * DIFF_HEAD_DIM)


def gla_chunkwise(q, k, v, log_f):
    B, L, H, DK = q.shape
    DV = v.shape[-1]
    n_chunks = L // CHUNK

    def to_chunks(t):
        return t.reshape(B, n_chunks, CHUNK, H, t.shape[-1]).transpose(1, 0, 3, 2, 4)

    mask = jnp.tril(jnp.ones((CHUNK, CHUNK), dtype=bool))[None, None, :, :, None]

    def step(S, inp):
        qb, kb, vb, gb = inp
        b = jnp.cumsum(gb, axis=2)
        rel = jnp.where(mask, b[:, :, :, None, :] - b[:, :, None, :, :], -jnp.inf)
        A = jnp.einsum('bhtk,bhsk,bhtsk->bhts', qb, kb, jnp.exp(rel))
        o = jnp.einsum('bhts,bhsv->bhtv', A, vb) + jnp.einsum('bhtk,bhkv->bhtv', qb * jnp.exp(b), S)
        b_last = b[:, :, -1:, :]
        S_new = jnp.exp(b_last[:, :, 0, :])[..., None] * S + jnp.einsum(
            'bhsk,bhsv->bhkv', kb * jnp.exp(b_last - b), vb)
        return S_new, o

    S0 = jnp.zeros((B, H, DK, DV), dtype=jnp.float32)
    _, o = lax.scan(step, S0, (to_chunks(q), to_chunks(k), to_chunks(v), to_chunks(log_f)))
    return o.transpose(1, 0, 3, 2, 4).reshape(B, L, H, DV)


def hgrn2_bidirectional(q_h, f_fw, f_bw, i_h, g_h, lb_fw, lb_bw, norm_w):
    B, L, _ = q_h.shape
    shp = (B, L, N_HGRN_HEADS, HGRN_HEAD_DIM)
    q = jax.nn.silu(q_h.astype(jnp.float32)).reshape(shp)
    v = i_h.astype(jnp.float32).reshape(shp)

    def gates(f_raw, lb):
        f = lb + (1.0 - lb) * jax.nn.sigmoid(f_raw.astype(jnp.float32))
        return (1.0 - f).reshape(shp), jnp.log(f).reshape(shp)

    k_fw, lf_fw = gates(f_fw, lb_fw)
    k_bw, lf_bw = gates(f_bw, lb_bw)
    o_fw = gla_chunkwise(q, k_fw, v, lf_fw)
    flip = lambda t: jnp.flip(t, axis=1)
    o_bw = flip(gla_chunkwise(flip(q), flip(k_bw), flip(v), flip(lf_bw)))
    o = rms_norm(o_fw + o_bw, norm_w) * jax.nn.silu(g_h.astype(jnp.float32)).reshape(shp)
    return o.reshape(B, L, HGRN_WIDTH).astype(q_h.dtype)


def encoder_trunk(x, attn_norm_w, w_in, diff_lambda, subln_w, hgrn_lb, hgrn_norm_w,
                  w_out, mlp_norm_w, w_up, w_down, final_norm_w):
    B, L, _ = x.shape
    cos, sin = rope_tables(L)
    lb_all = jnp.cumsum(jax.nn.softmax(hgrn_lb.astype(jnp.float32), axis=1), axis=1)
    bounds = [ATTN_WIDTH, 2 * ATTN_WIDTH, 3 * ATTN_WIDTH] + [3 * ATTN_WIDTH + j * HGRN_WIDTH for j in range(1, 5)]
    for layer in range(DEPTH):
        h = rms_norm(x, attn_norm_w[layer])
        proj = h @ w_in[layer]
        q_a, k_a, v_a, q_h, f_fw, f_bw, i_h, g_h = jnp.split(proj, bounds, axis=-1)
        lam_init = 0.8 - 0.6 * math.exp(-0.3 * layer)
        lq1, lk1, lq2, lk2 = diff_lambda[layer].astype(jnp.float32)
        lam = jnp.exp(jnp.sum(lq1 * lk1)) - jnp.exp(jnp.sum(lq2 * lk2)) + lam_init
        o_a = diff_attention(q_a, k_a, v_a, lam, cos, sin)
        o_a = (rms_norm(o_a, subln_w[layer]) * (1.0 - lam_init)).reshape(B, L, ATTN_WIDTH)
        o_h = hgrn2_bidirectional(q_h, f_fw, f_bw, i_h, g_h, lb_all[0, layer], lb_all[1, layer], hgrn_norm_w[layer])
        x = x + jnp.concatenate([o_a, o_h], axis=-1) @ w_out[layer]
        h = rms_norm(x, mlp_norm_w[layer])
        x = x + jnp.square(jax.nn.relu(h @ w_up[layer])) @ w_down[layer]
    return rms_norm(x, final_norm_w)


def setup_inputs(seed: int = 0) -> dict:
    key = jax.random.key(seed)
    ks = jax.random.split(key, 13)
    f32 = jnp.float32

    def normal(k, shape, scale):
        return jax.random.normal(k, shape, f32) * scale

    return {
        'x_prompt': normal(ks[0], (BATCH, SEQ, D_MODEL), 1.0),
        'x_sample': normal(ks[1], (DEC_BATCH, DEC_SEQ, D_MODEL), 1.0),
        'attn_norm_w': 1.0 + normal(ks[2], (DEPTH, D_MODEL), 0.01),
        'w_in': normal(ks[3], (DEPTH, D_MODEL, IN_PROJ_WIDTH), D_MODEL ** -0.5),
        'diff_lambda': normal(ks[4], (DEPTH, 4, DIFF_HEAD_DIM), 0.1),
        'subln_w': 1.0 + normal(ks[5], (DEPTH, 2 * DIFF_HEAD_DIM), 0.01),
        'hgrn_lb': normal(ks[6], (2, DEPTH + 1, HGRN_WIDTH), 0.1),
        'hgrn_norm_w': 1.0 + normal(ks[7], (DEPTH, HGRN_HEAD_DIM), 0.01),
        'w_out': normal(ks[8], (DEPTH, MIX_WIDTH, D_MODEL), MIX_WIDTH ** -0.5),
        'mlp_norm_w': 1.0 + normal(ks[9], (DEPTH, D_MODEL), 0.01),
        'w_up': normal(ks[10], (DEPTH, D_MODEL, D_FF), D_MODEL ** -0.5),
        'w_down': normal(ks[11], (DEPTH, D_FF, D_MODEL), D_FF ** -0.5),
        'final_norm_w': 1.0 + normal(ks[12], (D_MODEL,), 0.01),
    }


def reference(x_prompt, x_sample, attn_norm_w, w_in, diff_lambda, subln_w, hgrn_lb, hgrn_norm_w,
              w_out, mlp_norm_w, w_up, w_down, final_norm_w):
    y_prompt = encoder_trunk(x_prompt, attn_norm_w, w_in, diff_lambda, subln_w, hgrn_lb, hgrn_norm_w,
                             w_out, mlp_norm_w, w_up, w_down, final_norm_w)
    y_sample = encoder_trunk(x_sample, attn_norm_w, w_in, diff_lambda, subln_w, hgrn_lb, hgrn_norm_w,
                             w_out, mlp_norm_w, w_up, w_down, final_norm_w)
    return (y_prompt, y_sample)
```

```python
import functools
import math

import numpy as np
import jax
import jax.numpy as jnp
from jax import lax
from jax.experimental import pallas as pl
from jax.experimental.pallas import tpu as pltpu

F32 = jnp.float32
BF16 = jnp.bfloat16

D_MODEL = 4096
ATTN_WIDTH = 2048
HGRN_WIDTH = 2048
HEAD_DIM = 128
N_DIFF_HEADS = ATTN_WIDTH // (2 * HEAD_DIM)
N_HGRN_HEADS = HGRN_WIDTH // HEAD_DIM
D_FF = 4 * D_MODEL
ROPE_THETA = 10000.0
NORM_EPS = 1e-6
LAM_INIT = 0.8 - 0.6 * math.exp(-0.3 * 0)

LANES = 128
CHUNK = 64
N_LEVELS = 6
VMEM_LIMIT = 56 * 1024 * 1024

_NT = (((1,), (1,)), ((), ()))
_TN = (((0,), (0,)), ((), ()))


def _cparams(sem):
    return pltpu.CompilerParams(dimension_semantics=sem, vmem_limit_bytes=VMEM_LIMIT)


def _rmsnorm_kernel(x_ref, w_ref, o_ref):
    x = x_ref[...]
    ms = jnp.mean(x * x, axis=-1, keepdims=True)
    o_ref[...] = (x * lax.rsqrt(ms + NORM_EPS) * w_ref[...]).astype(o_ref.dtype)


def _rmsnorm(x, w, out_dtype):
    t, d = x.shape
    bt = min(256, t)
    return pl.pallas_call(
        _rmsnorm_kernel,
        grid=(t // bt,),
        in_specs=[pl.BlockSpec((bt, d), lambda i: (i, 0)),
                  pl.BlockSpec((1, d), lambda i: (0, 0))],
        out_specs=pl.BlockSpec((bt, d), lambda i: (i, 0)),
        out_shape=jax.ShapeDtypeStruct((t, d), out_dtype),
        compiler_params=_cparams(("parallel",)),
        name="rmsnorm",
    )(x, w.reshape(1, d))


def _inproj_rope_kernel(a_ref, w_ref, cos_ref, sin_ref, o_ref, *, n_rope_tiles):
    acc = jnp.dot(a_ref[...], w_ref[...], preferred_element_type=F32)
    j = pl.program_id(1)

    @pl.when(j < n_rope_tiles)
    def _():
        cos = cos_ref[...]
        sin = sin_ref[...]
        for g in range(acc.shape[1] // HEAD_DIM):
            blk = acc[:, g * HEAD_DIM:(g + 1) * HEAD_DIM]
            rot = pltpu.roll(blk, HEAD_DIM // 2, 1)
            o_ref[:, g * HEAD_DIM:(g + 1) * HEAD_DIM] = (blk * cos + rot * sin).astype(o_ref.dtype)

    @pl.when(j >= n_rope_tiles)
    def _():
        o_ref[...] = acc.astype(o_ref.dtype)


def _inproj_rope(a, w, cos, sin, rope_cols):
    t, k = a.shape
    n = w.shape[1]
    bm = min(1024, t)
    bn = 1024
    return pl.pallas_call(
        functools.partial(_inproj_rope_kernel, n_rope_tiles=rope_cols // bn),
        grid=(t // bm, n // bn),
        in_specs=[pl.BlockSpec((bm, k), lambda i, j: (i, 0)),
                  pl.BlockSpec((k, bn), lambda i, j: (0, j)),
                  pl.BlockSpec((bm, HEAD_DIM), lambda i, j: (i, 0)),
                  pl.BlockSpec((bm, HEAD_DIM), lambda i, j: (i, 0))],
        out_specs=pl.BlockSpec((bm, bn), lambda i, j: (i, j)),
        out_shape=jax.ShapeDtypeStruct((t, n), BF16),
        compiler_params=_cparams(("parallel", "arbitrary")),
        name="inproj_rope",
    )(a, w, cos, sin)


def _matmul_kernel(a_ref, w_ref, o_ref, *, relu2):
    acc = jnp.dot(a_ref[...], w_ref[...], preferred_element_type=F32)
    if relu2:
        acc = jnp.square(jnp.maximum(acc, 0.0))
    o_ref[...] = acc.astype(o_ref.dtype)


def _matmul(a, w, out_dtype, relu2=False, name="matmul"):
    t, k = a.shape
    n = w.shape[1]
    bm = min(1024, t)
    bn = 1024
    return pl.pallas_call(
        functools.partial(_matmul_kernel, relu2=relu2),
        grid=(t // bm, n // bn),
        in_specs=[pl.BlockSpec((bm, k), lambda i, j: (i, 0)),
                  pl.BlockSpec((k, bn), lambda i, j: (0, j))],
        out_specs=pl.BlockSpec((bm, bn), lambda i, j: (i, j)),
        out_shape=jax.ShapeDtypeStruct((t, n), out_dtype),
        compiler_params=_cparams(("parallel", "arbitrary")),
        name=name,
    )(a, w)


def _attn_kernel(dl_ref, sw_ref, q_ref, k_ref, v_ref, o_ref, m_ref, l_ref, acc_ref, *, nk):
    j = pl.program_id(3)
    c2 = (HEAD_DIM ** -0.5) * math.log2(math.e)

    @pl.when(j == 0)
    def _():
        m_ref[...] = jnp.full(m_ref.shape, -1e30, F32)
        l_ref[...] = jnp.zeros(l_ref.shape, F32)
        acc_ref[...] = jnp.zeros(acc_ref.shape, F32)

    v = v_ref[...]
    for c in range(2):
        q = q_ref[:, c * HEAD_DIM:(c + 1) * HEAD_DIM]
        k = k_ref[:, c * HEAD_DIM:(c + 1) * HEAD_DIM]
        s = lax.dot_general(q, k, _NT, preferred_element_type=F32)
        m_prev = m_ref[c]
        m_new = jnp.maximum(m_prev, jnp.max(s, axis=1, keepdims=True))
        alpha = jnp.exp2((m_prev - m_new) * c2)
        p = jnp.exp2(s * c2 - m_new[:, :1] * c2)
        l_ref[c] = alpha * l_ref[c] + jnp.sum(p, axis=1, keepdims=True)
        acc_ref[c] = alpha[:, :1] * acc_ref[c] + jnp.dot(p.astype(BF16), v, preferred_element_type=F32)
        m_ref[c] = m_new

    @pl.when(j == nk - 1)
    def _():
        dl = dl_ref[...]
        lam = (jnp.exp(jnp.sum(dl[0:1] * dl[1:2], axis=1, keepdims=True))
               - jnp.exp(jnp.sum(dl[2:3] * dl[3:4], axis=1, keepdims=True)) + LAM_INIT)
        o = acc_ref[0] / l_ref[0][:, :1] - lam * (acc_ref[1] / l_ref[1][:, :1])
        ms = jnp.mean(o * o, axis=1, keepdims=True)
        o_ref[...] = (o * lax.rsqrt(ms + NORM_EPS) * sw_ref[...] * (1.0 - LAM_INIT)).astype(o_ref.dtype)


def _diff_attention(proj, diff_lambda, subln_w, batch, seq):
    t = batch * seq
    tq = min(512, seq)
    tk = min(512, seq)
    nq, nk = seq // tq, seq // tk
    hw = 2 * HEAD_DIM
    return pl.pallas_call(
        functools.partial(_attn_kernel, nk=nk),
        grid=(batch, N_DIFF_HEADS, nq, nk),
        in_specs=[pl.BlockSpec((4, HEAD_DIM), lambda b, h, i, j: (0, 0)),
                  pl.BlockSpec((1, hw), lambda b, h, i, j: (0, 0)),
                  pl.BlockSpec((tq, hw), lambda b, h, i, j: (b * nq + i, h)),
                  pl.BlockSpec((tk, hw), lambda b, h, i, j: (b * nk + j, N_DIFF_HEADS + h)),
                  pl.BlockSpec((tk, hw), lambda b, h, i, j: (b * nk + j, 2 * N_DIFF_HEADS + h))],
        out_specs=pl.BlockSpec((tq, hw), lambda b, h, i, j: (b * nq + i, h)),
        out_shape=jax.ShapeDtypeStruct((t, ATTN_WIDTH), BF16),
        scratch_shapes=[pltpu.VMEM((2, tq, LANES), F32),
                        pltpu.VMEM((2, tq, LANES), F32),
                        pltpu.VMEM((2, tq, hw), F32)],
        compiler_params=_cparams(("parallel", "parallel", "parallel", "arbitrary")),
        name="diff_attention",
    )(diff_lambda, subln_w.reshape(1, hw), proj, proj, proj)


def _hgrn_constants(reverse):
    c = CHUNK
    sums = np.zeros((8, c, c), np.float32)
    valid = np.zeros((N_LEVELS + 1, c, c), np.float32)
    for lvl in range(N_LEVELS):
        h = c >> (lvl + 1)
        for t in range(c):
            mid = (t // (2 * h)) * 2 * h + h - 1
            if t % (2 * h) >= h:
                sums[lvl, t, mid + 1:t + 1] = 1.0
                valid[lvl, t, mid - h + 1:mid + 1] = 1.0
            else:
                sums[lvl, t, t + 1:mid + 1] = 1.0
    for t in range(c):
        sums[6, t, :t + 1] = 1.0
        sums[7, t, t + 1:] = 1.0
        valid[6, t, t] = 1.0
    if reverse:
        sums = sums[:, ::-1, ::-1]
        valid = valid[:, ::-1, ::-1]
    return (jnp.asarray(sums.reshape(8 * c, c), BF16), jnp.asarray(np.ascontiguousarray(valid), F32))


def _hgrn_kernel(*refs, reverse, final, n_chunks):
    if final:
        (lb_ref, sums_ref, valid_ref, q_ref, f_ref, v_ref, g_ref, prev_ref, nw_ref, o_ref, st_ref) = refs
    else:
        (lb_ref, sums_ref, valid_ref, q_ref, f_ref, v_ref, o_ref, st_ref) = refs

    @pl.when(pl.program_id(2) == 0)
    def _():
        st_ref[...] = jnp.zeros(st_ref.shape, F32)

    a0 = lb_ref[0:1, :]
    a1 = lb_ref[1:2, :]
    amax = jnp.maximum(a0, a1)
    e0 = jnp.exp(a0 - amax)
    lb = e0 / (e0 + jnp.exp(a1 - amax))
    sums = sums_ref[...]
    edge = 0 if reverse else CHUNK - 1

    def chunk(ci, carry):
        c = (n_chunks - 1 - ci) if reverse else ci
        rows = pl.ds(pl.multiple_of(c * CHUNK, CHUNK), CHUNK)
        qf = q_ref[rows, :].astype(F32)
        qs = qf * (1.0 / (1.0 + jnp.exp(-qf)))
        sig = 1.0 / (1.0 + jnp.exp(-f_ref[rows, :]))
        f = lb + (1.0 - lb) * sig
        kk = 1.0 - f
        logf = jnp.log(f)
        hi = logf.astype(BF16)
        lo = (logf - hi.astype(F32)).astype(BF16)
        expo = (jnp.dot(sums, hi, preferred_element_type=F32)
                + jnp.dot(sums, lo, preferred_element_type=F32))
        dec = jnp.exp(expo)
        v = v_ref[rows, :]

        a = valid_ref[N_LEVELS] * lax.dot_general(qs.astype(BF16), kk.astype(BF16), _NT,
                                                  preferred_element_type=F32)
        for lvl in range(N_LEVELS):
            d = dec[lvl * CHUNK:(lvl + 1) * CHUNK]
            a = a + valid_ref[lvl] * lax.dot_general((qs * d).astype(BF16), (kk * d).astype(BF16), _NT,
                                                     preferred_element_type=F32)
        d_in = dec[6 * CHUNK:7 * CHUNK]
        d_out = dec[7 * CHUNK:8 * CHUNK]
        st = st_ref[...]
        o = (jnp.dot(a.astype(BF16), v, preferred_element_type=F32)
             + lax.dot_general((qs * d_in).astype(BF16), st.astype(BF16), _NT, preferred_element_type=F32))
        st_ref[...] = st * d_in[edge:edge + 1, :] + lax.dot_general(
            v, (kk * d_out).astype(BF16), _TN, preferred_element_type=F32)

        if final:
            tot = prev_ref[rows, :] + o
            ms = jnp.mean(tot * tot, axis=1, keepdims=True)
            g = g_ref[rows, :].astype(F32)
            gate = g * (1.0 / (1.0 + jnp.exp(-g)))
            o_ref[rows, :] = (tot * lax.rsqrt(ms + NORM_EPS) * nw_ref[...] * gate).astype(o_ref.dtype)
        else:
            o_ref[rows, :] = o
        return carry

    lax.fori_loop(0, n_chunks, chunk, 0)


def _hgrn_pass(proj, fgate, lb_raw, batch, seq, *, reverse, q_col, f_col, v_col, g_col=None, prev=None,
               norm_w=None):
    final = prev is not None
    t = batch * seq
    tb = min(512, seq)
    nb = seq // tb
    n_chunks = tb // CHUNK
    sums, valid = _hgrn_constants(reverse)

    def row(b, j):
        return b * nb + ((nb - 1 - j) if reverse else j)

    tok = lambda col: pl.BlockSpec((tb, HEAD_DIM), lambda b, h, j: (row(b, j), col + h))
    in_specs = [pl.BlockSpec((2, HEAD_DIM), lambda b, h, j: (0, h)),
                pl.BlockSpec(sums.shape, lambda b, h, j: (0, 0)),
                pl.BlockSpec(valid.shape, lambda b, h, j: (0, 0, 0)),
                tok(q_col), tok(f_col), tok(v_col)]
    args = [lb_raw, sums, valid, proj, fgate, proj]
    if final:
        in_specs += [tok(g_col), tok(0), pl.BlockSpec((1, HEAD_DIM), lambda b, h, j: (0, 0))]
        args += [proj, prev, norm_w.reshape(1, HEAD_DIM)]
    return pl.pallas_call(
        functools.partial(_hgrn_kernel, reverse=reverse, final=final, n_chunks=n_chunks),
        grid=(batch, N_HGRN_HEADS, nb),
        in_specs=in_specs,
        out_specs=tok(0),
        out_shape=jax.ShapeDtypeStruct((t, HGRN_WIDTH), BF16 if final else F32),
        scratch_shapes=[pltpu.VMEM((HEAD_DIM, HEAD_DIM), F32)],
        compiler_params=_cparams(("parallel", "parallel", "arbitrary")),
        name="hgrn_bw_final" if final else "hgrn_fw",
    )(*args)


def _outproj_kernel(oa_ref, oh_ref, wa_ref, wh_ref, x_ref, o_ref):
    acc = jnp.dot(oa_ref[...], wa_ref[...], preferred_element_type=F32)
    acc = acc + jnp.dot(oh_ref[...], wh_ref[...], preferred_element_type=F32)
    o_ref[...] = x_ref[...] + acc


def _outproj(o_a, o_h, w_out, x):
    t = x.shape[0]
    bm = min(1024, t)
    bn = 1024
    return pl.pallas_call(
        _outproj_kernel,
        grid=(t // bm, D_MODEL // bn),
        in_specs=[pl.BlockSpec((bm, ATTN_WIDTH), lambda i, j: (i, 0)),
                  pl.BlockSpec((bm, HGRN_WIDTH), lambda i, j: (i, 0)),
                  pl.BlockSpec((ATTN_WIDTH, bn), lambda i, j: (0, j)),
                  pl.BlockSpec((HGRN_WIDTH, bn), lambda i, j: (1, j)),
                  pl.BlockSpec((bm, bn), lambda i, j: (i, j))],
        out_specs=pl.BlockSpec((bm, bn), lambda i, j: (i, j)),
        out_shape=jax.ShapeDtypeStruct((t, D_MODEL), F32),
        compiler_params=_cparams(("parallel", "arbitrary")),
        name="outproj_residual",
    )(o_a, o_h, w_out, w_out, x)


def _down_kernel(a_ref, w_ref, x_ref, o_ref, acc_ref, *, nk):
    kstep = pl.program_id(2)

    @pl.when(kstep == 0)
    def _():
        acc_ref[...] = jnp.zeros(acc_ref.shape, F32)

    acc_ref[...] += jnp.dot(a_ref[...], w_ref[...], preferred_element_type=F32)

    @pl.when(kstep == nk - 1)
    def _():
        o_ref[...] = x_ref[...] + acc_ref[...]


def _down_residual(a, w, x):
    t, k = a.shape
    n = w.shape[1]
    bm = min(1024, t)
    bn = 1024
    bk = 2048
    nk = k // bk
    return pl.pallas_call(
        functools.partial(_down_kernel, nk=nk),
        grid=(t // bm, n // bn, nk),
        in_specs=[pl.BlockSpec((bm, bk), lambda i, j, s: (i, s)),
                  pl.BlockSpec((bk, bn), lambda i, j, s: (s, j)),
                  pl.BlockSpec((bm, bn), lambda i, j, s: (i, j))],
        out_specs=pl.BlockSpec((bm, bn), lambda i, j, s: (i, j)),
        out_shape=jax.ShapeDtypeStruct((t, n), F32),
        scratch_shapes=[pltpu.VMEM((bm, bn), F32)],
        compiler_params=_cparams(("parallel", "parallel", "arbitrary")),
        name="mlp_down_residual",
    )(a, w, x)


def _rope_tables(batch, seq):
    inv = 1.0 / (ROPE_THETA ** (jnp.arange(0, HEAD_DIM, 2, dtype=F32) / HEAD_DIM))
    ang = jnp.arange(seq, dtype=F32)[:, None] * inv[None, :]
    ang = jnp.concatenate([ang, ang], axis=-1)
    sign = jnp.concatenate([-jnp.ones((HEAD_DIM // 2,), F32), jnp.ones((HEAD_DIM // 2,), F32)])
    cos = jnp.tile(jnp.cos(ang), (batch, 1))
    sin = jnp.tile(jnp.sin(ang) * sign, (batch, 1))
    return cos, sin


def _trunk(x3, p):
    batch, seq, d = x3.shape
    x = x3.reshape(batch * seq, d)
    cos, sin = _rope_tables(batch, seq)

    h = _rmsnorm(x, p["attn_norm_w"], BF16)
    proj = _inproj_rope(h, p["w_in_main"], cos, sin, rope_cols=2 * ATTN_WIDTH)
    fgate = _matmul(h, p["w_in_gate"], F32, name="inproj_gates")

    o_a = _diff_attention(proj, p["diff_lambda"], p["subln_w"], batch, seq)

    blk = lambda cols: cols // HEAD_DIM
    q_col, v_col, g_col = blk(3 * ATTN_WIDTH), blk(3 * ATTN_WIDTH + HGRN_WIDTH), blk(3 * ATTN_WIDTH + 2 * HGRN_WIDTH)
    o_fw = _hgrn_pass(proj, fgate, p["hgrn_lb"][0], batch, seq,
                      reverse=False, q_col=q_col, f_col=0, v_col=v_col)
    o_h = _hgrn_pass(proj, fgate, p["hgrn_lb"][1], batch, seq,
                     reverse=True, q_col=q_col, f_col=blk(HGRN_WIDTH), v_col=v_col, g_col=g_col,
                     prev=o_fw, norm_w=p["hgrn_norm_w"])

    x1 = _outproj(o_a, o_h, p["w_out"], x)
    h2 = _rmsnorm(x1, p["mlp_norm_w"], BF16)
    hidden = _matmul(h2, p["w_up"], BF16, relu2=True, name="mlp_up_relu2")
    x2 = _down_residual(hidden, p["w_down"], x1)
    y = _rmsnorm(x2, p["final_norm_w"], F32)
    return y.reshape(batch, seq, d)


def kernel(x_prompt, x_sample, attn_norm_w, w_in, diff_lambda, subln_w, hgrn_lb, hgrn_norm_w,
           w_out, mlp_norm_w, w_up, w_down, final_norm_w):
    w = w_in[0]
    gate_lo, gate_hi = 3 * ATTN_WIDTH + HGRN_WIDTH, 3 * ATTN_WIDTH + 3 * HGRN_WIDTH
    p = {
        "attn_norm_w": attn_norm_w[0],
        "w_in_main": jnp.concatenate([w[:, :gate_lo], w[:, gate_hi:]], axis=1).astype(BF16),
        "w_in_gate": w[:, gate_lo:gate_hi].astype(BF16),
        "diff_lambda": diff_lambda[0],
        "subln_w": subln_w[0],
        "hgrn_lb": hgrn_lb,
        "hgrn_norm_w": hgrn_norm_w[0],
        "w_out": w_out[0].astype(BF16),
        "mlp_norm_w": mlp_norm_w[0],
        "w_up": w_up[0].astype(BF16),
        "w_down": w_down[0].astype(BF16),
        "final_norm_w": final_norm_w,
    }
    return (_trunk(x_prompt, p), _trunk(x_sample, p))
```

```python
import functools
import math

import numpy as np
import jax
import jax.numpy as jnp
from jax import lax
from jax.experimental import pallas as pl
from jax.experimental.pallas import tpu as pltpu

F32 = jnp.float32
BF16 = jnp.bfloat16

D_MODEL = 4096
ATTN_WIDTH = 2048
HGRN_WIDTH = 2048
HEAD_DIM = 128
N_DIFF_HEADS = ATTN_WIDTH // (2 * HEAD_DIM)
N_HGRN_HEADS = HGRN_WIDTH // HEAD_DIM
D_FF = 4 * D_MODEL
ROPE_THETA = 10000.0
NORM_EPS = 1e-6
LAM_INIT = 0.8 - 0.6 * math.exp(-0.3 * 0)

LANES = 128
CHUNK = 64
N_LEVELS = 6
VMEM_LIMIT = 56 * 1024 * 1024

_NT = (((1,), (1,)), ((), ()))
_TN = (((0,), (0,)), ((), ()))


def _cparams(sem):
    return pltpu.CompilerParams(dimension_semantics=sem, vmem_limit_bytes=VMEM_LIMIT)


def _rmsnorm_kernel(x_ref, w_ref, o_ref):
    x = x_ref[...]
    ms = jnp.mean(x * x, axis=-1, keepdims=True)
    o_ref[...] = (x * lax.rsqrt(ms + NORM_EPS) * w_ref[...]).astype(o_ref.dtype)


def _rmsnorm(x, w, out_dtype):
    t, d = x.shape
    bt = min(256, t)
    return pl.pallas_call(
        _rmsnorm_kernel,
        grid=(t // bt,),
        in_specs=[pl.BlockSpec((bt, d), lambda i: (i, 0)),
                  pl.BlockSpec((1, d), lambda i: (0, 0))],
        out_specs=pl.BlockSpec((bt, d), lambda i: (i, 0)),
        out_shape=jax.ShapeDtypeStruct((t, d), out_dtype),
        compiler_params=_cparams(("parallel",)),
        name="rmsnorm",
    )(x, w.reshape(1, d))


def _inproj_rope_kernel(a_ref, w_ref, cos_ref, sin_ref, o_ref, *, n_rope_tiles):
    acc = jnp.dot(a_ref[...], w_ref[...], preferred_element_type=F32)
    j = pl.program_id(1)

    @pl.when(j < n_rope_tiles)
    def _():
        cos = cos_ref[...]
        sin = sin_ref[...]
        for g in range(acc.shape[1] // HEAD_DIM):
            blk = acc[:, g * HEAD_DIM:(g + 1) * HEAD_DIM]
            rot = pltpu.roll(blk, HEAD_DIM // 2, 1)
            o_ref[:, g * HEAD_DIM:(g + 1) * HEAD_DIM] = (blk * cos + rot * sin).astype(o_ref.dtype)

    @pl.when(j >= n_rope_tiles)
    def _():
        o_ref[...] = acc.astype(o_ref.dtype)


def _inproj_rope(a, w, cos, sin, rope_cols):
    t, k = a.shape
    n = w.shape[1]
    bm = min(1024, t)
    bn = 1024
    return pl.pallas_call(
        functools.partial(_inproj_rope_kernel, n_rope_tiles=rope_cols // bn),
        grid=(t // bm, n // bn),
        in_specs=[pl.BlockSpec((bm, k), lambda i, j: (i, 0)),
                  pl.BlockSpec((k, bn), lambda i, j: (0, j)),
                  pl.BlockSpec((bm, HEAD_DIM), lambda i, j: (i, 0)),
                  pl.BlockSpec((bm, HEAD_DIM), lambda i, j: (i, 0))],
        out_specs=pl.BlockSpec((bm, bn), lambda i, j: (i, j)),
        out_shape=jax.ShapeDtypeStruct((t, n), BF16),
        compiler_params=_cparams(("parallel", "arbitrary")),
        name="inproj_rope",
    )(a, w, cos, sin)


def _matmul_kernel(a_ref, w_ref, o_ref, *, relu2):
    acc = jnp.dot(a_ref[...], w_ref[...], preferred_element_type=F32)
    if relu2:
        acc = jnp.square(jnp.maximum(acc, 0.0))
    o_ref[...] = acc.astype(o_ref.dtype)


def _matmul(a, w, out_dtype, relu2=False, name="matmul"):
    t, k = a.shape
    n = w.shape[1]
    bm = min(1024, t)
    bn = 1024
    return pl.pallas_call(
        functools.partial(_matmul_kernel, relu2=relu2),
        grid=(t // bm, n // bn),
        in_specs=[pl.BlockSpec((bm, k), lambda i, j: (i, 0)),
                  pl.BlockSpec((k, bn), lambda i, j: (0, j))],
        out_specs=pl.BlockSpec((bm, bn), lambda i, j: (i, j)),
        out_shape=jax.ShapeDtypeStruct((t, n), out_dtype),
        compiler_params=_cparams(("parallel", "arbitrary")),
        name=name,
    )(a, w)


def _attn_kernel(dl_ref, sw_ref, q_ref, k_ref, v_ref, o_ref, m_ref, l_ref, acc_ref, *, nk):
    j = pl.program_id(3)
    c2 = (HEAD_DIM ** -0.5) * math.log2(math.e)

    @pl.when(j == 0)
    def _():
        m_ref[...] = jnp.full(m_ref.shape, -1e30, F32)
        l_ref[...] = jnp.zeros(l_ref.shape, F32)
        acc_ref[...] = jnp.zeros(acc_ref.shape, F32)

    v = v_ref[...]
    for c in range(2):
        q = q_ref[:, c * HEAD_DIM:(c + 1) * HEAD_DIM]
        k = k_ref[:, c * HEAD_DIM:(c + 1) * HEAD_DIM]
        s = lax.dot_general(q, k, _NT, preferred_element_type=F32)
        m_prev = m_ref[c]
        m_new = jnp.maximum(m_prev, jnp.max(s, axis=1, keepdims=True))
        alpha = jnp.exp2((m_prev - m_new) * c2)
        p = jnp.exp2(s * c2 - m_new[:, :1] * c2)
        l_ref[c] = alpha * l_ref[c] + jnp.sum(p, axis=1, keepdims=True)
        acc_ref[c] = alpha[:, :1] * acc_ref[c] + jnp.dot(p.astype(BF16), v, preferred_element_type=F32)
        m_ref[c] = m_new

    @pl.when(j == nk - 1)
    def _():
        dl = dl_ref[...]
        lam = (jnp.exp(jnp.sum(dl[0:1] * dl[1:2], axis=1, keepdims=True))
               - jnp.exp(jnp.sum(dl[2:3] * dl[3:4], axis=1, keepdims=True)) + LAM_INIT)
        o = acc_ref[0] / l_ref[0][:, :1] - lam * (acc_ref[1] / l_ref[1][:, :1])
        ms = jnp.mean(o * o, axis=1, keepdims=True)
        o_ref[...] = (o * lax.rsqrt(ms + NORM_EPS) * sw_ref[...] * (1.0 - LAM_INIT)).astype(o_ref.dtype)


def _diff_attention(proj, diff_lambda, subln_w, batch, seq):
    t = batch * seq
    tq = min(512, seq)
    tk = min(2048, seq)
    nq, nk = seq // tq, seq // tk
    hw = 2 * HEAD_DIM
    return pl.pallas_call(
        functools.partial(_attn_kernel, nk=nk),
        grid=(batch, N_DIFF_HEADS, nq, nk),
        in_specs=[pl.BlockSpec((4, HEAD_DIM), lambda b, h, i, j: (0, 0)),
                  pl.BlockSpec((1, hw), lambda b, h, i, j: (0, 0)),
                  pl.BlockSpec((tq, hw), lambda b, h, i, j: (b * nq + i, h)),
                  pl.BlockSpec((tk, hw), lambda b, h, i, j: (b * nk + j, N_DIFF_HEADS + h)),
                  pl.BlockSpec((tk, hw), lambda b, h, i, j: (b * nk + j, 2 * N_DIFF_HEADS + h))],
        out_specs=pl.BlockSpec((tq, hw), lambda b, h, i, j: (b * nq + i, h)),
        out_shape=jax.ShapeDtypeStruct((t, ATTN_WIDTH), BF16),
        scratch_shapes=[pltpu.VMEM((2, tq, LANES), F32),
                        pltpu.VMEM((2, tq, LANES), F32),
                        pltpu.VMEM((2, tq, hw), F32)],
        compiler_params=_cparams(("parallel", "parallel", "parallel", "arbitrary")),
        name="diff_attention",
    )(diff_lambda, subln_w.reshape(1, hw), proj, proj, proj)


def _hgrn_constants(reverse):
    c = CHUNK
    sums = np.zeros((8, c, c), np.float32)
    valid = np.zeros((N_LEVELS + 1, c, c), np.float32)
    for lvl in range(N_LEVELS):
        h = c >> (lvl + 1)
        for t in range(c):
            mid = (t // (2 * h)) * 2 * h + h - 1
            if t % (2 * h) >= h:
                sums[lvl, t, mid + 1:t + 1] = 1.0
                valid[lvl, t, mid - h + 1:mid + 1] = 1.0
            else:
                sums[lvl, t, t + 1:mid + 1] = 1.0
    for t in range(c):
        sums[6, t, :t + 1] = 1.0
        sums[7, t, t + 1:] = 1.0
        valid[6, t, t] = 1.0
    if reverse:
        sums = sums[:, ::-1, ::-1]
        valid = valid[:, ::-1, ::-1]
    return (jnp.asarray(sums.reshape(8 * c, c), BF16), jnp.asarray(np.ascontiguousarray(valid), F32))


def _hgrn_kernel(*refs, reverse, final, n_chunks):
    if final:
        (lb_ref, sums_ref, valid_ref, q_ref, f_ref, v_ref, g_ref, prev_ref, nw_ref, o_ref, st_ref) = refs
    else:
        (lb_ref, sums_ref, valid_ref, q_ref, f_ref, v_ref, o_ref, st_ref) = refs

    @pl.when(pl.program_id(2) == 0)
    def _():
        st_ref[...] = jnp.zeros(st_ref.shape, F32)

    a0 = lb_ref[0:1, :]
    a1 = lb_ref[1:2, :]
    amax = jnp.maximum(a0, a1)
    e0 = jnp.exp(a0 - amax)
    lb = e0 / (e0 + jnp.exp(a1 - amax))
    sums = sums_ref[...]
    edge = 0 if reverse else CHUNK - 1

    def chunk(ci, carry):
        c = (n_chunks - 1 - ci) if reverse else ci
        rows = pl.ds(pl.multiple_of(c * CHUNK, CHUNK), CHUNK)
        qf = q_ref[rows, :].astype(F32)
        qs = qf * (1.0 / (1.0 + jnp.exp(-qf)))
        sig = 1.0 / (1.0 + jnp.exp(-f_ref[rows, :]))
        f = lb + (1.0 - lb) * sig
        kk = 1.0 - f
        logf = jnp.log(f)
        hi = logf.astype(BF16)
        lo = (logf - hi.astype(F32)).astype(BF16)
        expo = (jnp.dot(sums, hi, preferred_element_type=F32)
                + jnp.dot(sums, lo, preferred_element_type=F32))
        dec = jnp.exp(expo)
        v = v_ref[rows, :]

        a = valid_ref[N_LEVELS] * lax.dot_general(qs.astype(BF16), kk.astype(BF16), _NT,
                                                  preferred_element_type=F32)
        for lvl in range(N_LEVELS):
            d = dec[lvl * CHUNK:(lvl + 1) * CHUNK]
            a = a + valid_ref[lvl] * lax.dot_general((qs * d).astype(BF16), (kk * d).astype(BF16), _NT,
                                                     preferred_element_type=F32)
        d_in = dec[6 * CHUNK:7 * CHUNK]
        d_out = dec[7 * CHUNK:8 * CHUNK]
        st = st_ref[...]
        o = (jnp.dot(a.astype(BF16), v, preferred_element_type=F32)
             + lax.dot_general((qs * d_in).astype(BF16), st.astype(BF16), _NT, preferred_element_type=F32))
        st_ref[...] = st * d_in[edge:edge + 1, :] + lax.dot_general(
            v, (kk * d_out).astype(BF16), _TN, preferred_element_type=F32)

        if final:
            tot = prev_ref[rows, :] + o
            ms = jnp.mean(tot * tot, axis=1, keepdims=True)
            g = g_ref[rows, :].astype(F32)
            gate = g * (1.0 / (1.0 + jnp.exp(-g)))
            o_ref[rows, :] = (tot * lax.rsqrt(ms + NORM_EPS) * nw_ref[...] * gate).astype(o_ref.dtype)
        else:
            o_ref[rows, :] = o
        return carry

    lax.fori_loop(0, n_chunks, chunk, 0, unroll=4)


def _hgrn_pass(proj, fgate, lb_raw, batch, seq, *, reverse, q_col, f_col, v_col, g_col=None, prev=None,
               norm_w=None):
    final = prev is not None
    t = batch * seq
    tb = min(512, seq)
    nb = seq // tb
    n_chunks = tb // CHUNK
    sums, valid = _hgrn_constants(reverse)

    def row(b, j):
        return b * nb + ((nb - 1 - j) if reverse else j)

    tok = lambda col: pl.BlockSpec((tb, HEAD_DIM), lambda b, h, j: (row(b, j), col + h))
    in_specs = [pl.BlockSpec((2, HEAD_DIM), lambda b, h, j: (0, h)),
                pl.BlockSpec(sums.shape, lambda b, h, j: (0, 0)),
                pl.BlockSpec(valid.shape, lambda b, h, j: (0, 0, 0)),
                tok(q_col), tok(f_col), tok(v_col)]
    args = [lb_raw, sums, valid, proj, fgate, proj]
    if final:
        in_specs += [tok(g_col), tok(0), pl.BlockSpec((1, HEAD_DIM), lambda b, h, j: (0, 0))]
        args += [proj, prev, norm_w.reshape(1, HEAD_DIM)]
    return pl.pallas_call(
        functools.partial(_hgrn_kernel, reverse=reverse, final=final, n_chunks=n_chunks),
        grid=(batch, N_HGRN_HEADS, nb),
        in_specs=in_specs,
        out_specs=tok(0),
        out_shape=jax.ShapeDtypeStruct((t, HGRN_WIDTH), BF16 if final else F32),
        scratch_shapes=[pltpu.VMEM((HEAD_DIM, HEAD_DIM), F32)],
        compiler_params=_cparams(("parallel", "parallel", "arbitrary")),
        name="hgrn_bw_final" if final else "hgrn_fw",
    )(*args)


def _outproj_kernel(oa_ref, oh_ref, wa_ref, wh_ref, x_ref, o_ref):
    acc = jnp.dot(oa_ref[...], wa_ref[...], preferred_element_type=F32)
    acc = acc + jnp.dot(oh_ref[...], wh_ref[...], preferred_element_type=F32)
    o_ref[...] = x_ref[...] + acc


def _outproj(o_a, o_h, w_out, x):
    t = x.shape[0]
    bm = min(1024, t)
    bn = 1024
    return pl.pallas_call(
        _outproj_kernel,
        grid=(t // bm, D_MODEL // bn),
        in_specs=[pl.BlockSpec((bm, ATTN_WIDTH), lambda i, j: (i, 0)),
                  pl.BlockSpec((bm, HGRN_WIDTH), lambda i, j: (i, 0)),
                  pl.BlockSpec((ATTN_WIDTH, bn), lambda i, j: (0, j)),
                  pl.BlockSpec((HGRN_WIDTH, bn), lambda i, j: (1, j)),
                  pl.BlockSpec((bm, bn), lambda i, j: (i, j))],
        out_specs=pl.BlockSpec((bm, bn), lambda i, j: (i, j)),
        out_shape=jax.ShapeDtypeStruct((t, D_MODEL), F32),
        compiler_params=_cparams(("parallel", "arbitrary")),
        name="outproj_residual",
    )(o_a, o_h, w_out, w_out, x)


def _down_kernel(a_ref, w_ref, x_ref, o_ref, acc_ref, *, nk):
    kstep = pl.program_id(2)

    @pl.when(kstep == 0)
    def _():
        acc_ref[...] = jnp.zeros(acc_ref.shape, F32)

    acc_ref[...] += jnp.dot(a_ref[...], w_ref[...], preferred_element_type=F32)

    @pl.when(kstep == nk - 1)
    def _():
        o_ref[...] = x_ref[...] + acc_ref[...]


def _down_residual(a, w, x):
    t, k = a.shape
    n = w.shape[1]
    bm = min(1024, t)
    bn = 1024
    bk = 2048
    nk = k // bk
    return pl.pallas_call(
        functools.partial(_down_kernel, nk=nk),
        grid=(t // bm, n // bn, nk),
        in_specs=[pl.BlockSpec((bm, bk), lambda i, j, s: (i, s)),
                  pl.BlockSpec((bk, bn), lambda i, j, s: (s, j)),
                  pl.BlockSpec((bm, bn), lambda i, j, s: (i, j))],
        out_specs=pl.BlockSpec((bm, bn), lambda i, j, s: (i, j)),
        out_shape=jax.ShapeDtypeStruct((t, n), F32),
        scratch_shapes=[pltpu.VMEM((bm, bn), F32)],
        compiler_params=_cparams(("parallel", "parallel", "arbitrary")),
        name="mlp_down_residual",
    )(a, w, x)


def _rope_tables(batch, seq):
    inv = 1.0 / (ROPE_THETA ** (jnp.arange(0, HEAD_DIM, 2, dtype=F32) / HEAD_DIM))
    ang = jnp.arange(seq, dtype=F32)[:, None] * inv[None, :]
    ang = jnp.concatenate([ang, ang], axis=-1)
    sign = jnp.concatenate([-jnp.ones((HEAD_DIM // 2,), F32), jnp.ones((HEAD_DIM // 2,), F32)])
    cos = jnp.tile(jnp.cos(ang), (batch, 1))
    sin = jnp.tile(jnp.sin(ang) * sign, (batch, 1))
    return cos, sin


def _trunk(x3, p):
    batch, seq, d = x3.shape
    x = x3.reshape(batch * seq, d)
    cos, sin = _rope_tables(batch, seq)

    h = _rmsnorm(x, p["attn_norm_w"], BF16)
    proj = _inproj_rope(h, p["w_in_main"], cos, sin, rope_cols=2 * ATTN_WIDTH)
    fgate = _matmul(h, p["w_in_gate"], F32, name="inproj_gates")

    o_a = _diff_attention(proj, p["diff_lambda"], p["subln_w"], batch, seq)

    blk = lambda cols: cols // HEAD_DIM
    q_col, v_col, g_col = blk(3 * ATTN_WIDTH), blk(3 * ATTN_WIDTH + HGRN_WIDTH), blk(3 * ATTN_WIDTH + 2 * HGRN_WIDTH)
    o_fw = _hgrn_pass(proj, fgate, p["hgrn_lb"][0], batch, seq,
                      reverse=False, q_col=q_col, f_col=0, v_col=v_col)
    o_h = _hgrn_pass(proj, fgate, p["hgrn_lb"][1], batch, seq,
                     reverse=True, q_col=q_col, f_col=blk(HGRN_WIDTH), v_col=v_col, g_col=g_col,
                     prev=o_fw, norm_w=p["hgrn_norm_w"])

    x1 = _outproj(o_a, o_h, p["w_out"], x)
    h2 = _rmsnorm(x1, p["mlp_norm_w"], BF16)
    hidden = _matmul(h2, p["w_up"], BF16, relu2=True, name="mlp_up_relu2")
    x2 = _down_residual(hidden, p["w_down"], x1)
    y = _rmsnorm(x2, p["final_norm_w"], F32)
    return y.reshape(batch, seq, d)


def kernel(x_prompt, x_sample, attn_norm_w, w_in, diff_lambda, subln_w, hgrn_lb, hgrn_norm_w,
           w_out, mlp_norm_w, w_up, w_down, final_norm_w):
    w = w_in[0]
    gate_lo, gate_hi = 3 * ATTN_WIDTH + HGRN_WIDTH, 3 * ATTN_WIDTH + 3 * HGRN_WIDTH
    p = {
        "attn_norm_w": attn_norm_w[0],
        "w_in_main": jnp.concatenate([w[:, :gate_lo], w[:, gate_hi:]], axis=1).astype(BF16),
        "w_in_gate": w[:, gate_lo:gate_hi].astype(BF16),
        "diff_lambda": diff_lambda[0],
        "subln_w": subln_w[0],
        "hgrn_lb": hgrn_lb,
        "hgrn_norm_w": hgrn_norm_w[0],
        "w_out": w_out[0].astype(BF16),
        "mlp_norm_w": mlp_norm_w[0],
        "w_up": w_up[0].astype(BF16),
        "w_down": w_down[0].astype(BF16),
        "final_norm_w": final_norm_w,
    }
    return (_trunk(x_prompt, p), _trunk(x_sample, p))
```

```python
import functools
import math

import numpy as np
import jax
import jax.numpy as jnp
from jax import lax
from jax.experimental import pallas as pl
from jax.experimental.pallas import tpu as pltpu

F32 = jnp.float32
BF16 = jnp.bfloat16

D_MODEL = 4096
ATTN_WIDTH = 2048
HGRN_WIDTH = 2048
HEAD_DIM = 128
N_DIFF_HEADS = ATTN_WIDTH // (2 * HEAD_DIM)
N_HGRN_HEADS = HGRN_WIDTH // HEAD_DIM
D_FF = 4 * D_MODEL
ROPE_THETA = 10000.0
NORM_EPS = 1e-6
LAM_INIT = 0.8 - 0.6 * math.exp(-0.3 * 0)

QK_EXP2_SCALE = (HEAD_DIM ** -0.5) * math.log2(math.e)
LANES = 128
CHUNK = 64
N_LEVELS = 6
VMEM_LIMIT = 56 * 1024 * 1024

_NT = (((1,), (1,)), ((), ()))
_TN = (((0,), (0,)), ((), ()))


def _cparams(sem):
    return pltpu.CompilerParams(dimension_semantics=sem, vmem_limit_bytes=VMEM_LIMIT)


def _rmsnorm_kernel(x_ref, w_ref, o_ref):
    x = x_ref[...]
    ms = jnp.mean(x * x, axis=-1, keepdims=True)
    o_ref[...] = (x * lax.rsqrt(ms + NORM_EPS) * w_ref[...]).astype(o_ref.dtype)


def _rmsnorm(x, w, out_dtype):
    t, d = x.shape
    bt = min(256, t)
    return pl.pallas_call(
        _rmsnorm_kernel,
        grid=(t // bt,),
        in_specs=[pl.BlockSpec((bt, d), lambda i: (i, 0)),
                  pl.BlockSpec((1, d), lambda i: (0, 0))],
        out_specs=pl.BlockSpec((bt, d), lambda i: (i, 0)),
        out_shape=jax.ShapeDtypeStruct((t, d), out_dtype),
        compiler_params=_cparams(("parallel",)),
        name="rmsnorm",
    )(x, w.reshape(1, d))


def _inproj_rope_kernel(a_ref, w_ref, cos_ref, sin_ref, o_ref, *, n_rope_tiles):
    acc = jnp.dot(a_ref[...], w_ref[...], preferred_element_type=F32)
    j = pl.program_id(1)

    @pl.when(j < n_rope_tiles)
    def _():
        qscale = jnp.where(j < n_rope_tiles // 2, QK_EXP2_SCALE, 1.0)
        cos = cos_ref[...] * qscale
        sin = sin_ref[...] * qscale
        for g in range(acc.shape[1] // HEAD_DIM):
            blk = acc[:, g * HEAD_DIM:(g + 1) * HEAD_DIM]
            rot = pltpu.roll(blk, HEAD_DIM // 2, 1)
            o_ref[:, g * HEAD_DIM:(g + 1) * HEAD_DIM] = (blk * cos + rot * sin).astype(o_ref.dtype)

    @pl.when(j >= n_rope_tiles)
    def _():
        o_ref[...] = acc.astype(o_ref.dtype)


def _inproj_rope(a, w, cos, sin, rope_cols):
    t, k = a.shape
    n = w.shape[1]
    bm = min(1024, t)
    bn = 1024
    return pl.pallas_call(
        functools.partial(_inproj_rope_kernel, n_rope_tiles=rope_cols // bn),
        grid=(t // bm, n // bn),
        in_specs=[pl.BlockSpec((bm, k), lambda i, j: (i, 0)),
                  pl.BlockSpec((k, bn), lambda i, j: (0, j)),
                  pl.BlockSpec((bm, HEAD_DIM), lambda i, j: (i, 0)),
                  pl.BlockSpec((bm, HEAD_DIM), lambda i, j: (i, 0))],
        out_specs=pl.BlockSpec((bm, bn), lambda i, j: (i, j)),
        out_shape=jax.ShapeDtypeStruct((t, n), BF16),
        compiler_params=_cparams(("parallel", "arbitrary")),
        name="inproj_rope",
    )(a, w, cos, sin)


def _matmul_kernel(a_ref, w_ref, o_ref, *, relu2):
    acc = jnp.dot(a_ref[...], w_ref[...], preferred_element_type=F32)
    if relu2:
        acc = jnp.square(jnp.maximum(acc, 0.0))
    o_ref[...] = acc.astype(o_ref.dtype)


def _matmul(a, w, out_dtype, relu2=False, name="matmul"):
    t, k = a.shape
    n = w.shape[1]
    bm = min(1024, t)
    bn = 1024
    return pl.pallas_call(
        functools.partial(_matmul_kernel, relu2=relu2),
        grid=(t // bm, n // bn),
        in_specs=[pl.BlockSpec((bm, k), lambda i, j: (i, 0)),
                  pl.BlockSpec((k, bn), lambda i, j: (0, j))],
        out_specs=pl.BlockSpec((bm, bn), lambda i, j: (i, j)),
        out_shape=jax.ShapeDtypeStruct((t, n), out_dtype),
        compiler_params=_cparams(("parallel", "arbitrary")),
        name=name,
    )(a, w)


def _attn_kernel(dl_ref, sw_ref, q_ref, k_ref, v_ref, o_ref, m_ref, l_ref, acc_ref, *, nk):
    j = pl.program_id(3)

    @pl.when(j == 0)
    def _():
        m_ref[...] = jnp.full(m_ref.shape, -1e30, F32)
        l_ref[...] = jnp.zeros(l_ref.shape, F32)
        acc_ref[...] = jnp.zeros(acc_ref.shape, F32)

    v = v_ref[...]
    for c in range(2):
        q = q_ref[:, c * HEAD_DIM:(c + 1) * HEAD_DIM]
        k = k_ref[:, c * HEAD_DIM:(c + 1) * HEAD_DIM]
        s = lax.dot_general(q, k, _NT, preferred_element_type=F32)
        m_prev = m_ref[c]
        m_new = jnp.maximum(m_prev, jnp.max(s, axis=1, keepdims=True))
        alpha = jnp.exp2(m_prev - m_new)
        p = jnp.exp2(s - m_new[:, :1])
        l_ref[c] = alpha * l_ref[c] + jnp.sum(p, axis=1, keepdims=True)
        acc_ref[c] = alpha[:, :1] * acc_ref[c] + jnp.dot(p.astype(BF16), v, preferred_element_type=F32)
        m_ref[c] = m_new

    @pl.when(j == nk - 1)
    def _():
        dl = dl_ref[...]
        lam = (jnp.exp(jnp.sum(dl[0:1] * dl[1:2], axis=1, keepdims=True))
               - jnp.exp(jnp.sum(dl[2:3] * dl[3:4], axis=1, keepdims=True)) + LAM_INIT)
        o = acc_ref[0] / l_ref[0][:, :1] - lam * (acc_ref[1] / l_ref[1][:, :1])
        ms = jnp.mean(o * o, axis=1, keepdims=True)
        o_ref[...] = (o * lax.rsqrt(ms + NORM_EPS) * sw_ref[...] * (1.0 - LAM_INIT)).astype(o_ref.dtype)


def _diff_attention(proj, diff_lambda, subln_w, batch, seq):
    t = batch * seq
    tq = min(512, seq)
    tk = min(2048, seq)
    nq, nk = seq // tq, seq // tk
    hw = 2 * HEAD_DIM
    return pl.pallas_call(
        functools.partial(_attn_kernel, nk=nk),
        grid=(batch, N_DIFF_HEADS, nq, nk),
        in_specs=[pl.BlockSpec((4, HEAD_DIM), lambda b, h, i, j: (0, 0)),
                  pl.BlockSpec((1, hw), lambda b, h, i, j: (0, 0)),
                  pl.BlockSpec((tq, hw), lambda b, h, i, j: (b * nq + i, h)),
                  pl.BlockSpec((tk, hw), lambda b, h, i, j: (b * nk + j, N_DIFF_HEADS + h)),
                  pl.BlockSpec((tk, hw), lambda b, h, i, j: (b * nk + j, 2 * N_DIFF_HEADS + h))],
        out_specs=pl.BlockSpec((tq, hw), lambda b, h, i, j: (b * nq + i, h)),
        out_shape=jax.ShapeDtypeStruct((t, ATTN_WIDTH), BF16),
        scratch_shapes=[pltpu.VMEM((2, tq, LANES), F32),
                        pltpu.VMEM((2, tq, LANES), F32),
                        pltpu.VMEM((2, tq, hw), F32)],
        compiler_params=_cparams(("parallel", "parallel", "parallel", "arbitrary")),
        name="diff_attention",
    )(diff_lambda, subln_w.reshape(1, hw), proj, proj, proj)


def _hgrn_constants(reverse):
    c = CHUNK
    sums = np.zeros((8, c, c), np.float32)
    valid = np.zeros((N_LEVELS + 1, c, c), np.float32)
    for lvl in range(N_LEVELS):
        h = c >> (lvl + 1)
        for t in range(c):
            mid = (t // (2 * h)) * 2 * h + h - 1
            if t % (2 * h) >= h:
                sums[lvl, t, mid + 1:t + 1] = 1.0
                valid[lvl, t, mid - h + 1:mid + 1] = 1.0
            else:
                sums[lvl, t, t + 1:mid + 1] = 1.0
    for t in range(c):
        sums[6, t, :t + 1] = 1.0
        sums[7, t, t + 1:] = 1.0
        valid[6, t, t] = 1.0
    if reverse:
        sums = sums[:, ::-1, ::-1]
        valid = valid[:, ::-1, ::-1]
    return (jnp.asarray(sums.reshape(8 * c, c), BF16), jnp.asarray(np.ascontiguousarray(valid), F32))


def _hgrn_kernel(*refs, reverse, final, n_chunks):
    if final:
        (lb_ref, sums_ref, valid_ref, q_ref, f_ref, v_ref, g_ref, prev_ref, nw_ref, o_ref, st_ref) = refs
    else:
        (lb_ref, sums_ref, valid_ref, q_ref, f_ref, v_ref, o_ref, st_ref) = refs

    @pl.when(pl.program_id(2) == 0)
    def _():
        st_ref[...] = jnp.zeros(st_ref.shape, F32)

    a0 = lb_ref[0:1, :]
    a1 = lb_ref[1:2, :]
    amax = jnp.maximum(a0, a1)
    e0 = jnp.exp(a0 - amax)
    lb = e0 / (e0 + jnp.exp(a1 - amax))
    sums = sums_ref[...]
    edge = 0 if reverse else CHUNK - 1

    def chunk(ci, carry):
        c = (n_chunks - 1 - ci) if reverse else ci
        rows = pl.ds(pl.multiple_of(c * CHUNK, CHUNK), CHUNK)
        qf = q_ref[rows, :].astype(F32)
        qs = qf * (1.0 / (1.0 + jnp.exp(-qf)))
        sig = 1.0 / (1.0 + jnp.exp(-f_ref[rows, :]))
        f = lb + (1.0 - lb) * sig
        kk = 1.0 - f
        logf = jnp.log(f)
        hi = logf.astype(BF16)
        lo = (logf - hi.astype(F32)).astype(BF16)
        expo = (jnp.dot(sums, hi, preferred_element_type=F32)
                + jnp.dot(sums, lo, preferred_element_type=F32))
        dec = jnp.exp(expo)
        v = v_ref[rows, :]

        a = valid_ref[N_LEVELS] * lax.dot_general(qs.astype(BF16), kk.astype(BF16), _NT,
                                                  preferred_element_type=F32)
        for lvl in range(N_LEVELS):
            d = dec[lvl * CHUNK:(lvl + 1) * CHUNK]
            a = a + valid_ref[lvl] * lax.dot_general((qs * d).astype(BF16), (kk * d).astype(BF16), _NT,
                                                     preferred_element_type=F32)
        d_in = dec[6 * CHUNK:7 * CHUNK]
        d_out = dec[7 * CHUNK:8 * CHUNK]
        st = st_ref[...]
        o = (jnp.dot(a.astype(BF16), v, preferred_element_type=F32)
             + lax.dot_general((qs * d_in).astype(BF16), st.astype(BF16), _NT, preferred_element_type=F32))
        st_ref[...] = st * d_in[edge:edge + 1, :] + lax.dot_general(
            v, (kk * d_out).astype(BF16), _TN, preferred_element_type=F32)

        if final:
            tot = prev_ref[rows, :] + o
            ms = jnp.mean(tot * tot, axis=1, keepdims=True)
            g = g_ref[rows, :].astype(F32)
            gate = g * (1.0 / (1.0 + jnp.exp(-g)))
            o_ref[rows, :] = (tot * lax.rsqrt(ms + NORM_EPS) * nw_ref[...] * gate).astype(o_ref.dtype)
        else:
            o_ref[rows, :] = o
        return carry

    lax.fori_loop(0, n_chunks, chunk, 0, unroll=4)


def _hgrn_pass(proj, fgate, lb_raw, batch, seq, *, reverse, q_col, f_col, v_col, g_col=None, prev=None,
               norm_w=None):
    final = prev is not None
    t = batch * seq
    tb = min(512, seq)
    nb = seq // tb
    n_chunks = tb // CHUNK
    sums, valid = _hgrn_constants(reverse)

    def row(b, j):
        return b * nb + ((nb - 1 - j) if reverse else j)

    tok = lambda col: pl.BlockSpec((tb, HEAD_DIM), lambda b, h, j: (row(b, j), col + h))
    in_specs = [pl.BlockSpec((2, HEAD_DIM), lambda b, h, j: (0, h)),
                pl.BlockSpec(sums.shape, lambda b, h, j: (0, 0)),
                pl.BlockSpec(valid.shape, lambda b, h, j: (0, 0, 0)),
                tok(q_col), tok(f_col), tok(v_col)]
    args = [lb_raw, sums, valid, proj, fgate, proj]
    if final:
        in_specs += [tok(g_col), tok(0), pl.BlockSpec((1, HEAD_DIM), lambda b, h, j: (0, 0))]
        args += [proj, prev, norm_w.reshape(1, HEAD_DIM)]
    return pl.pallas_call(
        functools.partial(_hgrn_kernel, reverse=reverse, final=final, n_chunks=n_chunks),
        grid=(batch, N_HGRN_HEADS, nb),
        in_specs=in_specs,
        out_specs=tok(0),
        out_shape=jax.ShapeDtypeStruct((t, HGRN_WIDTH), BF16 if final else F32),
        scratch_shapes=[pltpu.VMEM((HEAD_DIM, HEAD_DIM), F32)],
        compiler_params=_cparams(("parallel", "parallel", "arbitrary")),
        name="hgrn_bw_final" if final else "hgrn_fw",
    )(*args)


def _outproj_kernel(oa_ref, oh_ref, wa_ref, wh_ref, x_ref, o_ref):
    acc = jnp.dot(oa_ref[...], wa_ref[...], preferred_element_type=F32)
    acc = acc + jnp.dot(oh_ref[...], wh_ref[...], preferred_element_type=F32)
    o_ref[...] = x_ref[...] + acc


def _outproj(o_a, o_h, w_out, x):
    t = x.shape[0]
    bm = min(1024, t)
    bn = 1024
    return pl.pallas_call(
        _outproj_kernel,
        grid=(t // bm, D_MODEL // bn),
        in_specs=[pl.BlockSpec((bm, ATTN_WIDTH), lambda i, j: (i, 0)),
                  pl.BlockSpec((bm, HGRN_WIDTH), lambda i, j: (i, 0)),
                  pl.BlockSpec((ATTN_WIDTH, bn), lambda i, j: (0, j)),
                  pl.BlockSpec((HGRN_WIDTH, bn), lambda i, j: (1, j)),
                  pl.BlockSpec((bm, bn), lambda i, j: (i, j))],
        out_specs=pl.BlockSpec((bm, bn), lambda i, j: (i, j)),
        out_shape=jax.ShapeDtypeStruct((t, D_MODEL), F32),
        compiler_params=_cparams(("parallel", "arbitrary")),
        name="outproj_residual",
    )(o_a, o_h, w_out, w_out, x)


def _down_kernel(a_ref, w_ref, x_ref, o_ref, acc_ref, *, nk):
    kstep = pl.program_id(2)

    @pl.when(kstep == 0)
    def _():
        acc_ref[...] = jnp.zeros(acc_ref.shape, F32)

    acc_ref[...] += jnp.dot(a_ref[...], w_ref[...], preferred_element_type=F32)

    @pl.when(kstep == nk - 1)
    def _():
        o_ref[...] = x_ref[...] + acc_ref[...]


def _down_residual(a, w, x):
    t, k = a.shape
    n = w.shape[1]
    bm = min(1024, t)
    bn = 1024
    bk = 2048
    nk = k // bk
    return pl.pallas_call(
        functools.partial(_down_kernel, nk=nk),
        grid=(t // bm, n // bn, nk),
        in_specs=[pl.BlockSpec((bm, bk), lambda i, j, s: (i, s)),
                  pl.BlockSpec((bk, bn), lambda i, j, s: (s, j)),
                  pl.BlockSpec((bm, bn), lambda i, j, s: (i, j))],
        out_specs=pl.BlockSpec((bm, bn), lambda i, j, s: (i, j)),
        out_shape=jax.ShapeDtypeStruct((t, n), F32),
        scratch_shapes=[pltpu.VMEM((bm, bn), F32)],
        compiler_params=_cparams(("parallel", "parallel", "arbitrary")),
        name="mlp_down_residual",
    )(a, w, x)


def _rope_tables(batch, seq):
    inv = 1.0 / (ROPE_THETA ** (jnp.arange(0, HEAD_DIM, 2, dtype=F32) / HEAD_DIM))
    ang = jnp.arange(seq, dtype=F32)[:, None] * inv[None, :]
    ang = jnp.concatenate([ang, ang], axis=-1)
    sign = jnp.concatenate([-jnp.ones((HEAD_DIM // 2,), F32), jnp.ones((HEAD_DIM // 2,), F32)])
    cos = jnp.tile(jnp.cos(ang), (batch, 1))
    sin = jnp.tile(jnp.sin(ang) * sign, (batch, 1))
    return cos, sin


def _trunk(x3, p):
    batch, seq, d = x3.shape
    x = x3.reshape(batch * seq, d)
    cos, sin = _rope_tables(batch, seq)

    h = _rmsnorm(x, p["attn_norm_w"], BF16)
    proj = _inproj_rope(h, p["w_in_main"], cos, sin, rope_cols=2 * ATTN_WIDTH)
    fgate = _matmul(h, p["w_in_gate"], F32, name="inproj_gates")

    o_a = _diff_attention(proj, p["diff_lambda"], p["subln_w"], batch, seq)

    blk = lambda cols: cols // HEAD_DIM
    q_col, v_col, g_col = blk(3 * ATTN_WIDTH), blk(3 * ATTN_WIDTH + HGRN_WIDTH), blk(3 * ATTN_WIDTH + 2 * HGRN_WIDTH)
    o_fw = _hgrn_pass(proj, fgate, p["hgrn_lb"][0], batch, seq,
                      reverse=False, q_col=q_col, f_col=0, v_col=v_col)
    o_h = _hgrn_pass(proj, fgate, p["hgrn_lb"][1], batch, seq,
                     reverse=True, q_col=q_col, f_col=blk(HGRN_WIDTH), v_col=v_col, g_col=g_col,
                     prev=o_fw, norm_w=p["hgrn_norm_w"])

    x1 = _outproj(o_a, o_h, p["w_out"], x)
    h2 = _rmsnorm(x1, p["mlp_norm_w"], BF16)
    hidden = _matmul(h2, p["w_up"], BF16, relu2=True, name="mlp_up_relu2")
    x2 = _down_residual(hidden, p["w_down"], x1)
    y = _rmsnorm(x2, p["final_norm_w"], F32)
    return y.reshape(batch, seq, d)


def kernel(x_prompt, x_sample, attn_norm_w, w_in, diff_lambda, subln_w, hgrn_lb, hgrn_norm_w,
           w_out, mlp_norm_w, w_up, w_down, final_norm_w):
    w = w_in[0]
    gate_lo, gate_hi = 3 * ATTN_WIDTH + HGRN_WIDTH, 3 * ATTN_WIDTH + 3 * HGRN_WIDTH
    p = {
        "attn_norm_w": attn_norm_w[0],
        "w_in_main": jnp.concatenate([w[:, :gate_lo], w[:, gate_hi:]], axis=1).astype(BF16),
        "w_in_gate": w[:, gate_lo:gate_hi].astype(BF16),
        "diff_lambda": diff_lambda[0],
        "subln_w": subln_w[0],
        "hgrn_lb": hgrn_lb,
        "hgrn_norm_w": hgrn_norm_w[0],
        "w_out": w_out[0].astype(BF16),
        "mlp_norm_w": mlp_norm_w[0],
        "w_up": w_up[0].astype(BF16),
        "w_down": w_down[0].astype(BF16),
        "final_norm_w": final_norm_w,
    }
    return (_trunk(x_prompt, p), _trunk(x_sample, p))
```
